```python
import math
import jax
import jax.numpy as jnp
from jax import lax
import numpy as np

D_MODEL = 1024
BATCH = 8
SEQ = 8192
DEPTH = 2

GRID_W = 64
CTX_LEN = 256
EPS = 1e-6
LB_FLOOR = 1e-30

D_SSD = 1024
SSD_HEAD_DIM = 64
SSD_HEADS = D_SSD // SSD_HEAD_DIM
SSD_GROUPS = 2
SSD_HPG = SSD_HEADS // SSD_GROUPS
SSD_STATE = 128
SSD_CONV = 5
SSD_CHUNK = 128
SSD_XBC = D_SSD + 2 * SSD_GROUPS * SSD_STATE

D_HG = 512
HG_HEAD_DIM = 128
HG_HEADS = D_HG // HG_HEAD_DIM
HG_CHUNK = 64

D_HY = 512
HY_SHORT = 3
HY_BANDS = 16
HY_EMB = 1 + 2 * HY_BANDS
HY_FILT = 64
HY_MIN_DECAY = math.log(1e-2) / 1.5
HY_MAX_DECAY = math.log(1e-2) / 0.3

D_MIX = D_SSD + D_HG + D_HY
STATE_SIZES = (SSD_XBC, SSD_HEADS, SSD_HEADS, D_HG, D_HG, D_HG)
OUT_SIZES = (D_HG, 3 * D_HY, D_SSD, D_HG, D_HY)
IN_SIZES_ALL = STATE_SIZES + OUT_SIZES
STATE_COLS = sum(STATE_SIZES)
IN_COLS = STATE_COLS + sum(OUT_SIZES)

kernel_name = 'hybrid_ssd_hgrn2_hyena_prefix_dit'


def _offsets(sizes):
    return [int(v) for v in np.cumsum(sizes)[:-1]]


def _rms(x, w):
    xf = x.astype(jnp.float32)
    return xf * lax.rsqrt(jnp.mean(xf * xf, axis=-1, keepdims=True) + EPS) * w.astype(jnp.float32)


def _rmsnorm(x, w):
    return _rms(x, w).astype(x.dtype)


def _dwconv(x, w, b):
    width, ch = w.shape
    y = lax.conv_general_dilated(x, w.astype(x.dtype)[:, None, :], window_strides=(1,),
                                 padding=[(width // 2, width // 2)],
                                 dimension_numbers=('NWC', 'WIO', 'NWC'), feature_group_count=ch)
    return y + b.astype(x.dtype)


def _to_chunks(t, size):
    b, L = t.shape[0], t.shape[1]
    return jnp.moveaxis(t.reshape(b, L // size, size, *t.shape[2:]), 1, 0)


def _from_chunks(t):
    n, b, size = t.shape[0], t.shape[1], t.shape[2]
    return jnp.moveaxis(t, 0, 1).reshape(b, n * size, *t.shape[3:])


def _to_colmajor(t, rows):
    b = t.shape[0]
    return jnp.swapaxes(t.reshape(b, rows, GRID_W, *t.shape[2:]), 1, 2).reshape(t.shape)


def _from_colmajor(t, rows):
    b = t.shape[0]
    return jnp.swapaxes(t.reshape(b, GRID_W, rows, *t.shape[2:]), 1, 2).reshape(t.shape)


def _masked_decay(cum, mask):
    diff = cum[:, :, None] - cum[:, None]
    return jnp.where(mask, jnp.exp(jnp.where(mask, diff, 0.0)), 0.0)


def _ssd_scan(x, dt, a_neg, b_in, state, c_out=None):
    f32 = jnp.float32
    x, dt, b_in = x.astype(f32), dt.astype(f32), b_in.astype(f32)
    a = dt * a_neg.astype(f32)
    mask = jnp.tril(jnp.ones((SSD_CHUNK, SSD_CHUNK), bool))[None, :, :, None, None]

    def step(h, inp):
        if c_out is None:
            xc, dtc, ac, bc = inp
        else:
            xc, dtc, ac, bc, cc = inp
        cum = jnp.cumsum(ac, axis=1)
        total = cum[:, -1]
        xdt = xc * dtc[..., None]
        h_new = (jnp.exp(total)[..., None, None] * h
                 + jnp.einsum('bsgn,bsge,bsgep->bgepn', bc, jnp.exp(total[:, None] - cum), xdt))
        if c_out is None:
            return h_new, None
        decay = _masked_decay(cum, mask)
        scores = jnp.einsum('btgn,bsgn->btsg', cc, bc)
        y = (jnp.einsum('btsg,btsge,bsgep->btgep', scores, decay, xdt)
             + jnp.einsum('btgn,bgepn->btgep', cc, h) * jnp.exp(cum)[..., None])
        return h_new, y

    xs = (x, dt, a, b_in) + (() if c_out is None else (c_out.astype(f32),))
    h, ys = lax.scan(step, state, tuple(_to_chunks(t, SSD_CHUNK) for t in xs))
    return h, (None if c_out is None else _from_chunks(ys))


def _gla_scan(k, v, log_f, state, q=None):
    f32 = jnp.float32
    mask = jnp.tril(jnp.ones((HG_CHUNK, HG_CHUNK), bool))[None, :, :, None, None]

    def step(s, inp):
        if q is None:
            kc, vc, lc = inp
        else:
            kc, vc, lc, qc = inp
        cum = jnp.cumsum(lc, axis=1)
        total = cum[:, -1]
        s_new = (jnp.exp(total)[..., None] * s
                 + jnp.einsum('bshk,bshv->bhkv', kc * jnp.exp(total[:, None] - cum), vc))
        if q is None:
            return s_new, None
        decay = _masked_decay(cum, mask)
        attn = jnp.einsum('bthk,btshk,bshk->btsh', qc, decay, kc)
        o = (jnp.einsum('btsh,bshv->bthv', attn, vc)
             + jnp.einsum('bthk,bhkv->bthv', qc * jnp.exp(cum), s))
        return s_new, o

    xs = (k.astype(f32), v.astype(f32), log_f.astype(f32)) + (() if q is None else (q.astype(f32),))
    s, ys = lax.scan(step, state, tuple(_to_chunks(t, HG_CHUNK) for t in xs))
    return s, (None if q is None else _from_chunks(ys))


def _hgrn_forget(z, lb):
    bsz, L = z.shape[0], z.shape[1]
    zf = z.astype(jnp.float32).reshape(bsz, L, HG_HEADS, HG_HEAD_DIM)
    lb = lb.astype(jnp.float32).reshape(HG_HEADS, HG_HEAD_DIM)
    log_f = jnp.logaddexp(jnp.log(jnp.maximum(lb, LB_FLOOR)), jnp.log1p(-lb) + jax.nn.log_sigmoid(zf))
    k = (1.0 - lb) * jax.nn.sigmoid(-zf)
    return log_f, k


def _hyena_filter(L, p):
    f32 = jnp.float32
    t = jnp.linspace(0.0, 1.0, L, dtype=f32)[:, None]
    ang = ((2.0 * math.pi / L) * jnp.arange(L, dtype=f32)[:, None]
           * jnp.linspace(1e-4, HY_BANDS - 1, HY_BANDS, dtype=f32)[None, :])
    z = jnp.concatenate([t, jnp.cos(ang), -jnp.sin(ang)], axis=-1)
    freq = p['hy_filt_freq'].astype(f32)
    hdn = jnp.sin(freq * (z @ p['hy_filt_w1'].astype(f32) + p['hy_filt_b1'].astype(f32)))
    hdn = jnp.sin(freq * (hdn @ p['hy_filt_w2'].astype(f32) + p['hy_filt_b2'].astype(f32)))
    h = (hdn @ p['hy_filt_w3'].astype(f32)).reshape(L, 2, D_HY)
    deltas = jnp.abs(jnp.linspace(HY_MIN_DECAY, HY_MAX_DECAY, D_HY, dtype=f32))
    h = h * jnp.exp(-t * deltas)[:, None, :]
    kern = jnp.concatenate([h[:, 0], jnp.zeros((1, D_HY), f32), h[:0:-1, 1]], axis=0)
    return kern / (jnp.sum(jnp.abs(kern), axis=0, keepdims=True) + EPS)


def _fft_conv(u, kern):
    L = u.shape[1]
    uf = jnp.fft.rfft(u, n=2 * L, axis=1)
    kf = jnp.fft.rfft(kern, n=2 * L, axis=0)
    return jnp.fft.irfft(uf * kf, n=2 * L, axis=1)[:, :L]


def _token_mixers(u, p, lb, state_in, rows, with_output):
    f32 = jnp.float32
    bsz, L = u.shape[0], u.shape[1]
    parts = jnp.split(u, _offsets(IN_SIZES_ALL if with_output else STATE_SIZES), axis=-1)
    xbc, dtf_raw, dtb_raw, hg_i, hg_ff, hg_fb = parts[:6]
    rev = lambda t: jnp.flip(t, axis=1)

    xbc = jax.nn.silu(_dwconv(xbc, p['ssd_conv_w'], p['ssd_conv_b']))
    xs, b_in, c_out = jnp.split(xbc, [D_SSD, D_SSD + SSD_GROUPS * SSD_STATE], axis=-1)
    xs = xs.reshape(bsz, L, SSD_GROUPS, SSD_HPG, SSD_HEAD_DIM)
    b_in = b_in.reshape(bsz, L, SSD_GROUPS, SSD_STATE)
    c_out = c_out.reshape(bsz, L, SSD_GROUPS, SSD_STATE)
    a_neg = -jnp.exp(p['ssd_a_log'].astype(f32)).reshape(2, SSD_GROUPS, SSD_HPG)
    dt_f = jax.nn.softplus((dtf_raw + p['ssd_dt_bias'][0]).astype(f32)).reshape(bsz, L, SSD_GROUPS, SSD_HPG)
    dt_b = jax.nn.softplus((dtb_raw + p['ssd_dt_bias'][1]).astype(f32)).reshape(bsz, L, SSD_GROUPS, SSD_HPG)
    ssd_hf, ssd_yf = _ssd_scan(xs, dt_f, a_neg[0], b_in, state_in[0], c_out if with_output else None)
    ssd_hb, ssd_yb = _ssd_scan(rev(xs), rev(dt_b), a_neg[1], rev(b_in), state_in[1],
                               rev(c_out) if with_output else None)

    order = (lambda t: _to_colmajor(t, rows)) if rows is not None else (lambda t: t)
    lf_f, k_f = _hgrn_forget(hg_ff, lb[0])
    lf_b, k_b = _hgrn_forget(hg_fb, lb[1])
    v_hg = order(hg_i.reshape(bsz, L, HG_HEADS, HG_HEAD_DIM))
    lf_f, k_f, lf_b, k_b = (order(t) for t in (lf_f, k_f, lf_b, k_b))
    q_hg = order(parts[6].reshape(bsz, L, HG_HEADS, HG_HEAD_DIM)) if with_output else None
    hg_hf, hg_of = _gla_scan(k_f, v_hg, lf_f, state_in[2], q_hg)
    hg_hb, hg_ob = _gla_scan(rev(k_b), rev(v_hg), rev(lf_b), state_in[3],
                             rev(q_hg) if with_output else None)
    states = (ssd_hf, ssd_hb, hg_hf, hg_hb)
    if not with_output:
        return states, None
    hy_vxx, ssd_z, hg_g, hy_g = parts[7:]

    y_a = ssd_yf + rev(ssd_yb) + p['ssd_d'].astype(f32).reshape(SSD_GROUPS, SSD_HPG, 1) * xs.astype(f32)
    y_a = y_a.reshape(bsz, L, SSD_GROUPS, SSD_HPG * SSD_HEAD_DIM)
    y_a = y_a * jax.nn.silu(ssd_z.astype(f32)).reshape(bsz, L, SSD_GROUPS, SSD_HPG * SSD_HEAD_DIM)
    y_a = _rms(y_a, p['ssd_norm_w'].reshape(SSD_GROUPS, -1)).reshape(bsz, L, D_SSD)

    o = hg_of + rev(hg_ob)
    if rows is not None:
        o = _from_colmajor(o, rows)
    y_b = _rms(o, p['hg_norm_w'].reshape(HG_HEADS, HG_HEAD_DIM)).reshape(bsz, L, D_HG)
    y_b = y_b * jax.nn.silu(hg_g.astype(f32))

    hv, hx0, hx1 = jnp.split(_dwconv(hy_vxx, p['hy_conv_w'], p['hy_conv_b']), 3, axis=-1)
    w = (hx1 * hv).astype(f32)
    y_c = hx0.astype(f32) * (_fft_conv(w, _hyena_filter(L, p)) + p['hy_bias'].astype(f32) * w)
    y_c = y_c * jax.nn.silu(hy_g.astype(f32))

    return states, jnp.concatenate([y_a, y_b, y_c], axis=-1).astype(u.dtype)


def setup_inputs(seed: int = 0) -> dict:
    key = jax.random.key(seed)
    ks = jax.random.split(key, 28)
    f32 = jnp.float32
    nrm = lambda k, shape, s: s * jax.random.normal(k, shape, f32)
    dt = jnp.exp(jax.random.uniform(ks[10], (DEPTH, 2, SSD_HEADS), f32, math.log(1e-3), math.log(1e-1)))
    return {
        'x': nrm(ks[0], (BATCH, SEQ, D_MODEL), 1.0),
        'c': nrm(ks[1], (BATCH, D_MODEL), 1.0),
        'ctx': nrm(ks[2], (BATCH, CTX_LEN, D_MODEL), 1.0),
        'c_ctx': nrm(ks[3], (D_MODEL,), 1.0),
        'ada_w': nrm(ks[4], (DEPTH, D_MODEL, 3 * D_MODEL), 0.5 * D_MODEL ** -0.5),
        'ada_b': nrm(ks[5], (DEPTH, 3 * D_MODEL), 0.02),
        'norm_w': 1.0 + nrm(ks[6], (DEPTH, D_MODEL), 0.05),
        'w_in': nrm(ks[7], (DEPTH, D_MODEL, IN_COLS), D_MODEL ** -0.5),
        'ssd_conv_w': nrm(ks[8], (DEPTH, SSD_CONV, SSD_XBC), SSD_CONV ** -0.5),
        'ssd_conv_b': nrm(ks[9], (DEPTH, SSD_XBC), 0.02),
        'ssd_dt_bias': dt + jnp.log(-jnp.expm1(-dt)),
        'ssd_a_log': jnp.log(jax.random.uniform(ks[11], (DEPTH, 2, SSD_HEADS), f32, 1.0, 16.0)),
        'ssd_d': 1.0 + nrm(ks[12], (DEPTH, SSD_HEADS), 0.1),
        'ssd_norm_w': 1.0 + nrm(ks[13], (DEPTH, D_SSD), 0.05),
        'hg_lb_logits': 1.0 + nrm(ks[14], (2, DEPTH, D_HG), 0.5),
        'hg_norm_w': 1.0 + nrm(ks[15], (DEPTH, D_HG), 0.05),
        'hy_conv_w': nrm(ks[16], (DEPTH, HY_SHORT, 3 * D_HY), HY_SHORT ** -0.5),
        'hy_conv_b': nrm(ks[17], (DEPTH, 3 * D_HY), 0.02),
        'hy_filt_w1': nrm(ks[18], (DEPTH, HY_EMB, HY_FILT), HY_EMB ** -0.5),
        'hy_filt_b1': nrm(ks[19], (DEPTH, HY_FILT), 0.1),
        'hy_filt_w2': nrm(ks[20], (DEPTH, HY_FILT, HY_FILT), HY_FILT ** -0.5),
        'hy_filt_b2': nrm(ks[21], (DEPTH, HY_FILT), 0.1),
        'hy_filt_freq': 1.0 + nrm(ks[22], (DEPTH, HY_FILT), 0.1),
        'hy_filt_w3': nrm(ks[23], (DEPTH, HY_FILT, 2 * D_HY), HY_FILT ** -0.5),
        'hy_bias': nrm(ks[24], (DEPTH, D_HY), 1.0),
        'w_out': nrm(ks[25], (DEPTH, D_MIX, D_MODEL), D_MIX ** -0.5),
        'final_norm_w': 1.0 + nrm(ks[26], (D_MODEL,), 0.05),
    }


def reference(x, c, ctx, c_ctx, ada_w, ada_b, norm_w, w_in, ssd_conv_w, ssd_conv_b, ssd_dt_bias,
              ssd_a_log, ssd_d, ssd_norm_w, hg_lb_logits, hg_norm_w, hy_conv_w, hy_conv_b,
              hy_filt_w1, hy_filt_b1, hy_filt_w2, hy_filt_b2, hy_filt_freq, hy_filt_w3, hy_bias,
              w_out, final_norm_w):
    f32 = jnp.float32
    bsz, seq = x.shape[0], x.shape[1]
    rows = seq // GRID_W
    lb_p = jax.nn.softmax(hg_lb_logits.astype(f32), axis=1)
    lb_all = jnp.cumsum(lb_p, axis=1) - lb_p[:, :1]
    zero_states = ((jnp.zeros((bsz, SSD_GROUPS, SSD_HPG, SSD_HEAD_DIM, SSD_STATE), f32),) * 2
                   + (jnp.zeros((bsz, HG_HEADS, HG_HEAD_DIM, HG_HEAD_DIM), f32),) * 2)
    s_lat = jax.nn.silu(c)
    s_ctx = jax.nn.silu(c_ctx)
    xc = ctx
    for l in range(DEPTH):
        last = l == DEPTH - 1
        p = {'ssd_conv_w': ssd_conv_w[l], 'ssd_conv_b': ssd_conv_b[l], 'ssd_dt_bias': ssd_dt_bias[l],
             'ssd_a_log': ssd_a_log[l], 'ssd_d': ssd_d[l], 'ssd_norm_w': ssd_norm_w[l],
             'hg_norm_w': hg_norm_w[l], 'hy_conv_w': hy_conv_w[l], 'hy_conv_b': hy_conv_b[l],
             'hy_filt_w1': hy_filt_w1[l], 'hy_filt_b1': hy_filt_b1[l], 'hy_filt_w2': hy_filt_w2[l],
             'hy_filt_b2': hy_filt_b2[l], 'hy_filt_freq': hy_filt_freq[l], 'hy_filt_w3': hy_filt_w3[l],
             'hy_bias': hy_bias[l]}
        shift, scale, gate = jnp.split(s_lat @ ada_w[l] + ada_b[l], 3, axis=-1)
        shift_c, scale_c, gate_c = jnp.split(s_ctx @ ada_w[l] + ada_b[l], 3, axis=-1)
        hc = _rmsnorm(xc, norm_w[l]) * (1.0 + scale_c) + shift_c
        w_in_c = w_in[l][:, :STATE_COLS] if last else w_in[l]
        ctx_states, yc = _token_mixers(hc @ w_in_c, p, lb_all[:, l], zero_states, None, not last)
        h = _rmsnorm(x, norm_w[l]) * (1.0 + scale[:, None]) + shift[:, None]
        _, y = _token_mixers(h @ w_in[l], p, lb_all[:, l], ctx_states, rows, True)
        x = x + gate[:, None] * (y @ w_out[l])
        if not last:
            xc = xc + gate_c * (yc @ w_out[l])
    return _rmsnorm(x, final_norm_w)
```

```python
import functools
import math

import jax
import jax.numpy as jnp
from jax import lax
from jax.experimental import pallas as pl
from jax.experimental.pallas import tpu as pltpu

F32 = jnp.float32
BF16 = jnp.bfloat16

D_MODEL = 1024
GRID_W = 64
EPS = 1e-6
LB_FLOOR = 1e-30

D_SSD = 1024
SSD_HEAD_DIM = 64
SSD_HEADS = 16
SSD_GROUPS = 2
SSD_HPG = 8
SSD_STATE = 128
SSD_CONV = 5
SSD_CHUNK = 128
SSD_XBC = D_SSD + 2 * SSD_GROUPS * SSD_STATE
SSD_GW = SSD_HPG * SSD_HEAD_DIM

D_HG = 512
HG_HEAD_DIM = 128
HG_HEADS = 4
HG_CHUNK = 64
HG_SUB = 16

D_HY = 512
HY_SHORT = 3
HY_BANDS = 16
HY_EMB = 1 + 2 * HY_BANDS
HY_FILT = 64
HY_MIN_DECAY = math.log(1e-2) / 1.5
HY_MAX_DECAY = math.log(1e-2) / 0.3
FFT_N2 = 128
DIRECT_DFT_MAX_L = 512

D_MIX = D_SSD + D_HG + D_HY
DT_PAD = 128
HG_COLS = 4 * D_HG
GT_COLS = D_SSD + D_HG + D_HY
HY_COLS = 3 * D_HY
IN_COLS_PERM = SSD_XBC + HY_COLS + HG_COLS + GT_COLS + DT_PAD

LANE = 128
ROW_TILE = 256
NEG_BIG = -1e30
VMEM_LIMIT = 56 * 1024 * 1024


def _cparams(n_axes, vmem=None):
    return pltpu.CompilerParams(dimension_semantics=("arbitrary",) * n_axes,
                                vmem_limit_bytes=vmem)


def _bdot(a, b):
    return jnp.dot(a.astype(BF16), b.astype(BF16), preferred_element_type=F32)


def _bdot_nt(a, b):
    return lax.dot_general(a.astype(BF16), b.astype(BF16), (((1,), (1,)), ((), ())),
                           preferred_element_type=F32)


def _bdot_tn(a, b):
    return lax.dot_general(a.astype(BF16), b.astype(BF16), (((0,), (0,)), ((), ())),
                           preferred_element_type=F32)


def _hdot(a, b):
    return jnp.dot(a, b, precision=lax.Precision.HIGHEST, preferred_element_type=F32)


def _split3(a):
    a1 = a.astype(BF16)
    r1 = a - a1.astype(F32)
    a2 = r1.astype(BF16)
    a3 = (r1 - a2.astype(F32)).astype(BF16)
    return a1, a2, a3


def _tri_dot(tri, a):
    a1, a2, a3 = _split3(a)
    d = functools.partial(jnp.dot, preferred_element_type=F32)
    return d(tri, a1) + d(tri, a2) + d(tri, a3)


def _tri_dot_nt(a, tri):
    a1, a2, a3 = _split3(a)
    d = functools.partial(lax.dot_general, dimension_numbers=(((1,), (1,)), ((), ())),
                          preferred_element_type=F32)
    return d(a1, tri) + d(a2, tri) + d(a3, tri)


def _silu(x):
    return x * jax.nn.sigmoid(x)


def _tri_mask(n, rev):
    r = lax.broadcasted_iota(jnp.int32, (n, n), 0)
    s = lax.broadcasted_iota(jnp.int32, (n, n), 1)
    return (s >= r) if rev else (s <= r)


def _adaln_kernel(c_ref, w_ref, b_ref, o_ref):
    s = _silu(c_ref[...])
    o_ref[0] = _hdot(s, w_ref[0]) + b_ref[0]


def _adaln(cvec, ada_w, ada_b):
    depth, d, d3 = ada_w.shape
    nrow = cvec.shape[0]
    return pl.pallas_call(
        _adaln_kernel,
        out_shape=jax.ShapeDtypeStruct((depth, nrow, d3), F32),
        grid=(depth, d3 // d),
        in_specs=[pl.BlockSpec((nrow, d), lambda l, j: (0, 0)),
                  pl.BlockSpec((1, d, d), lambda l, j: (l, 0, j)),
                  pl.BlockSpec((1, 1, d), lambda l, j: (l, 0, j))],
        out_specs=pl.BlockSpec((1, nrow, d), lambda l, j: (l, 0, j)),
        compiler_params=_cparams(2),
        name="adaln",
    )(cvec, ada_w, ada_b.reshape(depth, 1, d3))


_IN_OUT_WIDTHS = (SSD_XBC, HY_COLS, HG_COLS, GT_COLS, DT_PAD)
_IN_CHUNK = 512


def _inproj_kernel(x_ref, nw_ref, sc_ref, sh_ref, w_ref, *o_refs):
    x = x_ref[...]
    ms = jnp.mean(x * x, axis=-1, keepdims=True)
    h = x * lax.rsqrt(ms + EPS) * nw_ref[...]
    h = h * (1.0 + sc_ref[0]) + sh_ref[0]
    hb = h.astype(BF16)
    col = 0
    for o_ref in o_refs:
        n = o_ref.shape[1]
        for j in range(0, n, _IN_CHUNK):
            w = min(_IN_CHUNK, n - j)
            o_ref[:, j:j + w] = jnp.dot(hb, w_ref[:, col + j:col + j + w], preferred_element_type=F32)
        col += n


def _permute_w_in(w):
    o_dt = SSD_XBC
    o_hg = o_dt + 2 * SSD_HEADS
    o_hy = o_hg + HG_COLS
    o_gt = o_hy + HY_COLS
    pad = jnp.zeros((w.shape[0], DT_PAD - 2 * SSD_HEADS), w.dtype)
    wp = jnp.concatenate([w[:, :SSD_XBC], w[:, o_hy:o_gt], w[:, o_hg:o_hy], w[:, o_gt:o_gt + GT_COLS],
                          w[:, o_dt:o_hg], pad], axis=1)
    return wp.astype(BF16)


def _inproj(x2d, norm_w, scale, shift, wp, seq_len):
    m, d = x2d.shape
    tm = ROW_TILE
    nb = scale.shape[0]
    per = seq_len // tm
    mod_map = (lambda i: (0, 0, 0)) if nb == 1 else (lambda i: (i // per, 0, 0))
    return pl.pallas_call(
        _inproj_kernel,
        out_shape=[jax.ShapeDtypeStruct((m, n), F32) for n in _IN_OUT_WIDTHS],
        grid=(m // tm,),
        in_specs=[pl.BlockSpec((tm, d), lambda i: (i, 0)),
                  pl.BlockSpec((1, d), lambda i: (0, 0)),
                  pl.BlockSpec((1, 1, d), mod_map),
                  pl.BlockSpec((1, 1, d), mod_map),
                  pl.BlockSpec((d, IN_COLS_PERM), lambda i: (0, 0), pipeline_mode=pl.Buffered(1))],
        out_specs=[pl.BlockSpec((tm, n), lambda i: (i, 0)) for n in _IN_OUT_WIDTHS],
        compiler_params=_cparams(1, VMEM_LIMIT),
        name="inproj",
    )(x2d, norm_w.reshape(1, d), scale, shift, wp)


_CONV_ROWS = 256
_CONV_HALO = 8


def _fill_padded(x_ref, pad_ref, seq_len):
    lanes = pad_ref.shape[1]
    pad_ref[0:_CONV_HALO, :] = jnp.zeros((_CONV_HALO, lanes), F32)
    pad_ref[seq_len + _CONV_HALO:seq_len + 2 * _CONV_HALO, :] = jnp.zeros((_CONV_HALO, lanes), F32)

    def cp(i, c):
        base = pl.multiple_of(i * _CONV_ROWS, _CONV_ROWS)
        pad_ref[pl.ds(base + _CONV_HALO, _CONV_ROWS), :] = x_ref[0, pl.ds(base, _CONV_ROWS), :]
        return c
    lax.fori_loop(0, seq_len // _CONV_ROWS, cp, 0)


def _conv_tile(pad_ref, w_ref, b_ref, base, width):
    half = width // 2
    acc = None
    for k in range(width):
        term = pad_ref[pl.ds(base + (_CONV_HALO + k - half), _CONV_ROWS), :] * w_ref[k:k + 1, :]
        acc = term if acc is None else acc + term
    return acc + b_ref[...]


def _ssd_conv_kernel(x_ref, w_ref, b_ref, o_ref, pad_ref, *, seq_len):
    _fill_padded(x_ref, pad_ref, seq_len)

    def body(i, c):
        base = pl.multiple_of(i * _CONV_ROWS, _CONV_ROWS)
        o_ref[0, pl.ds(base, _CONV_ROWS), :] = _silu(_conv_tile(pad_ref, w_ref, b_ref, base, SSD_CONV))
        return c
    lax.fori_loop(0, seq_len // _CONV_ROWS, body, 0)


def _ssd_conv(xbc, w, b):
    bsz, seq_len, ch = xbc.shape
    return pl.pallas_call(
        functools.partial(_ssd_conv_kernel, seq_len=seq_len),
        out_shape=jax.ShapeDtypeStruct(xbc.shape, F32),
        grid=(bsz, ch // LANE),
        in_specs=[pl.BlockSpec((1, seq_len, LANE), lambda b_, j: (b_, 0, j)),
                  pl.BlockSpec((SSD_CONV, LANE), lambda b_, j: (0, j)),
                  pl.BlockSpec((1, LANE), lambda b_, j: (0, j))],
        out_specs=pl.BlockSpec((1, seq_len, LANE), lambda b_, j: (b_, 0, j)),
        scratch_shapes=[pltpu.VMEM((seq_len + 2 * _CONV_HALO, LANE), F32)],
        compiler_params=_cparams(2, VMEM_LIMIT),
        name="ssd_conv",
    )(xbc, w, b.reshape(1, ch))


def _hy_conv_kernel(v_ref, x0_ref, x1_ref, wv_ref, w0_ref, w1_ref, bv_ref, b0_ref, b1_ref,
                    wprod_ref, hx0_ref, pv_ref, p0_ref, p1_ref, *, seq_len):
    _fill_padded(v_ref, pv_ref, seq_len)
    _fill_padded(x0_ref, p0_ref, seq_len)
    _fill_padded(x1_ref, p1_ref, seq_len)

    def body(i, c):
        base = pl.multiple_of(i * _CONV_ROWS, _CONV_ROWS)
        hv = _conv_tile(pv_ref, wv_ref, bv_ref, base, HY_SHORT)
        hx1 = _conv_tile(p1_ref, w1_ref, b1_ref, base, HY_SHORT)
        wprod_ref[0, pl.ds(base, _CONV_ROWS), :] = hx1 * hv
        hx0_ref[0, pl.ds(base, _CONV_ROWS), :] = _conv_tile(p0_ref, w0_ref, b0_ref, base, HY_SHORT)
        return c
    lax.fori_loop(0, seq_len // _CONV_ROWS, body, 0)


def _hy_conv(hy, w, b):
    bsz, seq_len, _ = hy.shape
    nj = D_HY // LANE
    b2 = b.reshape(1, 3 * D_HY)
    xs = lambda g: pl.BlockSpec((1, seq_len, LANE), lambda b_, j, g=g: (b_, 0, g * nj + j))
    ws = lambda g: pl.BlockSpec((HY_SHORT, LANE), lambda b_, j, g=g: (0, g * nj + j))
    bs = lambda g: pl.BlockSpec((1, LANE), lambda b_, j, g=g: (0, g * nj + j))
    out = jax.ShapeDtypeStruct((bsz, seq_len, D_HY), F32)
    ospec = pl.BlockSpec((1, seq_len, LANE), lambda b_, j: (b_, 0, j))
    pad = pltpu.VMEM((seq_len + 2 * _CONV_HALO, LANE), F32)
    return pl.pallas_call(
        functools.partial(_hy_conv_kernel, seq_len=seq_len),
        out_shape=[out, out],
        grid=(bsz, nj),
        in_specs=[xs(0), xs(1), xs(2), ws(0), ws(1), ws(2), bs(0), bs(1), bs(2)],
        out_specs=[ospec, ospec],
        scratch_shapes=[pad, pad, pad],
        compiler_params=_cparams(2, VMEM_LIMIT),
        name="hy_conv",
    )(hy, hy, hy, w, w, w, b2, b2, b2)


def _ssd_kernel(xbc_ref, dt_ref, dtt_ref, prow_ref, pcol_ref, h0_ref, y_ref, hout_ref, h_scr, *, rev, nc):
    c = pl.program_id(1)

    @pl.when(c == 0)
    def _init():
        h_scr[...] = h0_ref[0]

    ch = SSD_CHUNK
    prow = prow_ref[...]
    pcol = pcol_ref[...]
    dt = jax.nn.softplus(dt_ref[0] + prow[0:1, :])
    a = dt * (-jnp.exp(prow[1:2, :]))
    dtt = jax.nn.softplus(dtt_ref[0] + pcol[:, 0:1])
    at = dtt * (-jnp.exp(pcol[:, 1:2]))
    mask = _tri_mask(ch, rev)
    tri = mask.astype(F32).astype(BF16)
    cum = _tri_dot(tri, a)
    cumt = _tri_dot_nt(at, tri)
    tot = cum[0:1, :] if rev else cum[ch - 1:ch, :]
    ecum = jnp.exp(cum)
    wst = jnp.exp(tot - cum)
    etot = jnp.exp(tot)

    for g in range(SSD_GROUPS):
        b_g = xbc_ref[0, :, D_SSD + g * SSD_STATE:D_SSD + (g + 1) * SSD_STATE]
        c_g = xbc_ref[0, :, D_SSD + (SSD_GROUPS + g) * SSD_STATE:D_SSD + (SSD_GROUPS + g + 1) * SSD_STATE]
        c_gb = c_g.astype(BF16)
        scores = _bdot_nt(c_gb, b_g)
        h_t = h_scr[g]
        y_inter = _bdot(c_gb, h_t)
        b_gt = b_g.T.astype(BF16)
        for j in range(SSD_HPG):
            e = g * SSD_HPG + j
            lo, hi = e * SSD_HEAD_DIM, (e + 1) * SSD_HEAD_DIM
            diff = cum[:, e:e + 1] - cumt[e:e + 1, :]
            dec = jnp.exp(jnp.where(mask, diff, NEG_BIG))
            xdt = xbc_ref[0, :, lo:hi] * dt[:, e:e + 1]
            y_e = _bdot(scores * dec, xdt)
            y_e = y_e + y_inter[:, j * SSD_HEAD_DIM:(j + 1) * SSD_HEAD_DIM] * ecum[:, e:e + 1]
            y_ref[0, :, lo:hi] = y_e
            upd = _bdot(b_gt, xdt * wst[:, e:e + 1])
            h_scr[g, :, j * SSD_HEAD_DIM:(j + 1) * SSD_HEAD_DIM] = (
                h_t[:, j * SSD_HEAD_DIM:(j + 1) * SSD_HEAD_DIM] * etot[0:1, e:e + 1] + upd)

    @pl.when(c == nc - 1)
    def _fin():
        hout_ref[0] = h_scr[...]


def _ssd_scan(xbc_act, dt_raw, dt_bias, a_log, h0, rev):
    bsz, seq_len, _ = xbc_act.shape
    nc = seq_len // SSD_CHUNK
    cidx = (lambda c: nc - 1 - c) if rev else (lambda c: c)
    prow = jnp.stack([dt_bias, a_log], axis=0)
    pcol = prow.T
    dtt = jnp.swapaxes(dt_raw, 1, 2)
    hshape = (bsz, SSD_GROUPS, SSD_STATE, SSD_GW)
    hspec = pl.BlockSpec((1, SSD_GROUPS, SSD_STATE, SSD_GW), lambda b_, c: (b_, 0, 0, 0))
    return pl.pallas_call(
        functools.partial(_ssd_kernel, rev=rev, nc=nc),
        out_shape=[jax.ShapeDtypeStruct((bsz, seq_len, D_SSD), F32), jax.ShapeDtypeStruct(hshape, F32)],
        grid=(bsz, nc),
        in_specs=[pl.BlockSpec((1, SSD_CHUNK, SSD_XBC), lambda b_, c: (b_, cidx(c), 0)),
                  pl.BlockSpec((1, SSD_CHUNK, SSD_HEADS), lambda b_, c: (b_, cidx(c), 0)),
                  pl.BlockSpec((1, SSD_HEADS, SSD_CHUNK), lambda b_, c: (b_, 0, cidx(c))),
                  pl.BlockSpec((2, SSD_HEADS), lambda b_, c: (0, 0)),
                  pl.BlockSpec((SSD_HEADS, 2), lambda b_, c: (0, 0)),
                  hspec],
        out_specs=[pl.BlockSpec((1, SSD_CHUNK, D_SSD), lambda b_, c: (b_, cidx(c), 0)), hspec],
        scratch_shapes=[pltpu.VMEM((SSD_GROUPS, SSD_STATE, SSD_GW), F32)],
        compiler_params=_cparams(2),
        name="ssd_scan_bwd" if rev else "ssd_scan_fwd",
    )(xbc_act, dt_raw, dtt, prow, pcol, h0)


def _gla_kernel(v_ref, z_ref, q_ref, lbl_ref, s0_ref, o_ref, sout_ref, s_scr, *, rev, nc, layer):
    c = pl.program_id(1)

    @pl.when(c == 0)
    def _init():
        s_scr[...] = s0_ref[0]

    ch = HG_CHUNK
    z = z_ref[0]
    v = v_ref[0]
    q = q_ref[0]
    lg = lbl_ref[...]
    ex = jnp.exp(lg - jnp.max(lg, axis=0, keepdims=True))
    p = ex / jnp.sum(ex, axis=0, keepdims=True)
    lb = jnp.zeros((1, D_HG), F32)
    for l in range(1, layer + 1):
        lb = lb + p[l:l + 1, :]
    log_f = jnp.logaddexp(jnp.log(jnp.maximum(lb, LB_FLOOR)), jnp.log1p(-lb) + jax.nn.log_sigmoid(z))
    k = (1.0 - lb) * jax.nn.sigmoid(-z)

    mask = _tri_mask(ch, rev)
    tri = mask.astype(F32).astype(BF16)
    cum = _tri_dot(tri, log_f)
    tot = cum[0:1, :] if rev else cum[ch - 1:ch, :]
    qe = q * jnp.exp(cum)
    kd = k * jnp.exp(tot - cum)
    etot = jnp.exp(tot)

    nsub = ch // HG_SUB
    for h in range(HG_HEADS):
        sl = slice(h * HG_HEAD_DIM, (h + 1) * HG_HEAD_DIM)
        s_t = s_scr[h]
        o_inter = _bdot_nt(qe[:, sl], s_t)
        for i in range(nsub):
            r0, r1 = i * HG_SUB, (i + 1) * HG_SUB
            if rev:
                k0, k1 = r0, ch
                c0 = cum[r1 - 1:r1, sl]
            else:
                k0, k1 = 0, r1
                c0 = cum[r0:r0 + 1, sl]
            qp = q[r0:r1, sl] * jnp.exp(cum[r0:r1, sl] - c0)
            kp = k[k0:k1, sl] * jnp.exp(c0 - cum[k0:k1, sl])
            att = _bdot_nt(qp, kp)
            rr = lax.broadcasted_iota(jnp.int32, att.shape, 0) + r0
            cc = lax.broadcasted_iota(jnp.int32, att.shape, 1) + k0
            att = jnp.where((cc >= rr) if rev else (cc <= rr), att, 0.0)
            o_ref[0, r0:r1, sl] = _bdot(att, v[k0:k1, sl]) + o_inter[r0:r1, :]
        s_scr[h] = s_t * etot[:, sl] + _bdot_tn(v[:, sl], kd[:, sl])

    @pl.when(c == nc - 1)
    def _fin():
        sout_ref[0] = s_scr[...]


def _gla_scan(hg, lb_logits, s0, rows, width, layer, rev):
    bsz, seq_len, _ = hg.shape
    nr = rows // HG_CHUNK
    nc = nr * width
    hg3 = hg.reshape(bsz, rows, width * HG_COLS)
    cidx = (lambda c: nc - 1 - c) if rev else (lambda c: c)

    def spec(j):
        return pl.BlockSpec((1, HG_CHUNK, D_HG),
                            lambda b_, c, j=j: (b_, cidx(c) % nr, (cidx(c) // nr) * 4 + j))
    sshape = (bsz, HG_HEADS, HG_HEAD_DIM, HG_HEAD_DIM)
    sspec = pl.BlockSpec((1, HG_HEADS, HG_HEAD_DIM, HG_HEAD_DIM), lambda b_, c: (b_, 0, 0, 0))
    depth = lb_logits.shape[0]
    o, s_fin = pl.pallas_call(
        functools.partial(_gla_kernel, rev=rev, nc=nc, layer=layer),
        out_shape=[jax.ShapeDtypeStruct((bsz, rows, width * D_HG), F32), jax.ShapeDtypeStruct(sshape, F32)],
        grid=(bsz, nc),
        in_specs=[spec(0), spec(2 if rev else 1), spec(3),
                  pl.BlockSpec((depth, D_HG), lambda b_, c: (0, 0)), sspec],
        out_specs=[pl.BlockSpec((1, HG_CHUNK, D_HG), lambda b_, c: (b_, cidx(c) % nr, cidx(c) // nr)), sspec],
        scratch_shapes=[pltpu.VMEM((HG_HEADS, HG_HEAD_DIM, HG_HEAD_DIM), F32)],
        compiler_params=_cparams(2),
        name="gla_scan_bwd" if rev else "gla_scan_fwd",
    )(hg3, hg3, hg3, lb_logits, s0)
    return o.reshape(bsz, seq_len, D_HG), s_fin


def _filter_kernel(bands_ref, w1_ref, b1_ref, w2_ref, b2_ref, fr_ref, w3_ref, dl_ref, h_ref, asum_ref,
                   *, seq_len, rt):
    i = pl.program_id(0)
    pos = (lax.broadcasted_iota(jnp.int32, (rt, 1), 0) + i * rt).astype(F32)
    t = pos * (1.0 / (seq_len - 1))
    ang = ((2.0 * math.pi / seq_len) * pos) * bands_ref[...]
    freq = fr_ref[...]
    pre = (t * w1_ref[0:1, :] + _hdot(jnp.cos(ang), w1_ref[1:1 + HY_BANDS, :])
           + _hdot(-jnp.sin(ang), w1_ref[1 + HY_BANDS:HY_EMB, :]) + b1_ref[...])
    hdn = jnp.sin(freq * pre)
    hdn = jnp.sin(freq * (_hdot(hdn, w2_ref[...]) + b2_ref[...]))
    h = _hdot(hdn, w3_ref[...])
    decay = jnp.exp(-t * dl_ref[...])
    hf = h[:, :D_HY] * decay
    hb = h[:, D_HY:] * decay
    h_ref[:, :D_HY] = hf
    h_ref[:, D_HY:] = hb

    @pl.when(i == 0)
    def _init():
        asum_ref[...] = jnp.zeros_like(asum_ref)
    hb_taps = jnp.where(pos >= 1.0, jnp.abs(hb), 0.0)
    asum_ref[:, :D_HY] += jnp.sum(jnp.abs(hf), axis=0, keepdims=True)
    asum_ref[:, D_HY:] += jnp.sum(hb_taps, axis=0, keepdims=True)


def _hyena_filter(seq_len, w1, b1, w2, b2, freq, w3):
    rt = min(512, seq_len)
    bands = jnp.linspace(1e-4, HY_BANDS - 1, HY_BANDS, dtype=F32).reshape(1, HY_BANDS)
    deltas = jnp.abs(jnp.linspace(HY_MIN_DECAY, HY_MAX_DECAY, D_HY, dtype=F32)).reshape(1, D_HY)
    full = lambda shape: pl.BlockSpec(shape, lambda i: (0,) * len(shape))
    h, asum = pl.pallas_call(
        functools.partial(_filter_kernel, seq_len=seq_len, rt=rt),
        out_shape=[jax.ShapeDtypeStruct((seq_len, 2 * D_HY), F32), jax.ShapeDtypeStruct((1, 2 * D_HY), F32)],
        grid=(seq_len // rt,),
        in_specs=[full((1, HY_BANDS)), full((HY_EMB, HY_FILT)), full((1, HY_FILT)), full((HY_FILT, HY_FILT)),
                  full((1, HY_FILT)), full((1, HY_FILT)), full((HY_FILT, 2 * D_HY)), full((1, D_HY))],
        out_specs=[pl.BlockSpec((rt, 2 * D_HY), lambda i: (i, 0)), full((1, 2 * D_HY))],
        compiler_params=_cparams(1),
        name="hyena_filter",
    )(bands, w1, b1.reshape(1, -1), w2, b2.reshape(1, -1), freq.reshape(1, -1), w3, deltas)
    kern = jnp.concatenate([h[:, :D_HY], jnp.zeros((1, D_HY), F32), jnp.flip(h[1:, D_HY:], axis=0)], axis=0)
    return kern, asum


def _cos_sin(num, den):
    ang = (2.0 * math.pi / den) * jnp.mod(num, den).astype(F32)
    return jnp.cos(ang), jnp.sin(ang)


def _leftmul_kernel(m_ref, x_ref, o_ref):
    o_ref[0] = jnp.dot(m_ref[...], x_ref[0].astype(BF16), preferred_element_type=F32)


def _leftmul(mat, x):
    npair, kdim, cols = x.shape
    r = mat.shape[0]
    tc = min(cols, 4096)
    return pl.pallas_call(
        _leftmul_kernel,
        out_shape=jax.ShapeDtypeStruct((npair, r, cols), F32),
        grid=(npair, cols // tc),
        in_specs=[pl.BlockSpec((r, kdim), lambda p, j: (0, 0)),
                  pl.BlockSpec((1, kdim, tc), lambda p, j: (p, 0, j))],
        out_specs=pl.BlockSpec((1, r, tc), lambda p, j: (p, 0, j)),
        compiler_params=_cparams(2),
        name="dft_outer",
    )(mat, x)


def _filt_spectrum_kernel(g_ref, y_ref, asum_ref, kf_ref):
    n2 = FFT_N2
    yk = jnp.concatenate([y_ref[0, 0, 0], y_ref[0, 1, 0]], axis=0)
    z = _bdot(g_ref[0], yk)
    inv = 1.0 / (asum_ref[:, :D_HY] + asum_ref[:, D_HY:] + EPS)
    kf_ref[0, 0] = z[:n2] * inv
    kf_ref[0, 1] = z[n2:] * inv


def _spectral_conv_kernel(g_ref, gi_ref, kf_ref, y_ref, o_ref):
    n2 = FFT_N2
    yk = jnp.concatenate([y_ref[0, 0, 0], y_ref[0, 1, 0]], axis=0)
    z = _bdot(g_ref[0], yk)
    zr, zi = z[:n2], z[n2:]
    kr, ki = kf_ref[0, 0], kf_ref[0, 1]
    pk = jnp.concatenate([zr * kr - zi * ki, zr * ki + zi * kr], axis=0)
    qk = _bdot(gi_ref[0], pk)
    o_ref[0, 0, 0] = qk[:n2]
    o_ref[0, 1, 0] = qk[n2:]


def _fft_conv(wprod, kern, asum):
    bsz, seq_len, ch = wprod.shape
    n = 2 * seq_len
    n2 = FFT_N2
    n1 = n // n2
    half = n1 // 2
    npair = bsz // 2
    ii = lax.broadcasted_iota(jnp.int32, (n1, n1), 0)
    jj = lax.broadcasted_iota(jnp.int32, (n1, n1), 1)
    f1r, f1i_pos = _cos_sin(ii * jj, n1)
    f1i = -f1i_pos
    mat_a = jnp.concatenate([jnp.concatenate([f1r[:, :half], -f1i[:, :half]], axis=1),
                             jnp.concatenate([f1i[:, :half], f1r[:, :half]], axis=1)], axis=0).astype(BF16)
    mat_k = jnp.concatenate([f1r, f1i], axis=0).astype(BF16)
    mat_o = (jnp.concatenate([jnp.concatenate([f1r[:half], f1i[:half]], axis=1),
                              jnp.concatenate([-f1i[:half], f1r[:half]], axis=1)], axis=0) * (1.0 / n)).astype(BF16)
    k1 = lax.broadcasted_iota(jnp.int32, (n1, n2, n2), 0)
    k2 = lax.broadcasted_iota(jnp.int32, (n1, n2, n2), 1)
    i2 = lax.broadcasted_iota(jnp.int32, (n1, n2, n2), 2)
    gr, gi_pos = _cos_sin(i2 * (k1 + n1 * k2), n)
    gi = -gi_pos
    g_mat = jnp.concatenate([jnp.concatenate([gr, -gi], axis=2),
                             jnp.concatenate([gi, gr], axis=2)], axis=1).astype(BF16)
    hr = jnp.swapaxes(gr, 1, 2)
    hi = -jnp.swapaxes(gi, 1, 2)
    gi_mat = jnp.concatenate([jnp.concatenate([hr, -hi], axis=2),
                              jnp.concatenate([hi, hr], axis=2)], axis=1).astype(BF16)

    gspec1 = pl.BlockSpec((1, 2 * n2, 2 * n2), lambda a: (a, 0, 0))
    yk = _leftmul(mat_k, kern.reshape(1, n1, n2 * ch)).reshape(1, 2, n1, n2, ch)
    kf = pl.pallas_call(
        _filt_spectrum_kernel,
        out_shape=jax.ShapeDtypeStruct((n1, 2, n2, ch), F32),
        grid=(n1,),
        in_specs=[gspec1,
                  pl.BlockSpec((1, 2, 1, n2, ch), lambda a: (0, 0, a, 0, 0)),
                  pl.BlockSpec((1, 2 * D_HY), lambda a: (0, 0))],
        out_specs=pl.BlockSpec((1, 2, n2, ch), lambda a: (a, 0, 0, 0)),
        compiler_params=_cparams(1),
        name="filter_spectrum",
    )(g_mat, yk, asum)

    y = _leftmul(mat_a, wprod.reshape(npair, n1, n2 * ch)).reshape(npair, 2, n1, n2, ch)
    gspec = pl.BlockSpec((1, 2 * n2, 2 * n2), lambda a, p: (a, 0, 0))
    yspec = pl.BlockSpec((1, 2, 1, n2, ch), lambda a, p: (p, 0, a, 0, 0))
    qv = pl.pallas_call(
        _spectral_conv_kernel,
        out_shape=jax.ShapeDtypeStruct((npair, 2, n1, n2, ch), F32),
        grid=(n1, npair),
        in_specs=[gspec, gspec, pl.BlockSpec((1, 2, n2, ch), lambda a, p: (a, 0, 0, 0)), yspec],
        out_specs=yspec,
        compiler_params=_cparams(2),
        name="spectral_conv",
    )(g_mat, gi_mat, kf, y)
    out = _leftmul(mat_o, qv.reshape(npair, 2 * n1, n2 * ch))
    return out.reshape(bsz, seq_len, ch)


def _dft_conv_kernel(ff_ref, fk_ref, fi_ref, kern_ref, asum_ref, w_ref, o_ref, *, n):
    kf = _bdot(fk_ref[...], kern_ref[...]) * (1.0 / (asum_ref[:, :D_HY] + asum_ref[:, D_HY:] + EPS))
    kr, ki = kf[:n], kf[n:]
    z = _bdot(ff_ref[...], w_ref[0])
    zr, zi = z[:n], z[n:]
    pk = jnp.concatenate([zr * kr - zi * ki, zr * ki + zi * kr], axis=0)
    o_ref[0] = _bdot(fi_ref[...], pk)


def _dft_conv(wprod, kern, asum):
    bsz, seq_len, ch = wprod.shape
    n = 2 * seq_len
    npair = bsz // 2
    kk = lax.broadcasted_iota(jnp.int32, (n, n), 0)
    nn = lax.broadcasted_iota(jnp.int32, (n, n), 1)
    cr, sr = _cos_sin(kk * nn, n)
    fr, fi = cr[:, :seq_len], -sr[:, :seq_len]
    ffwd = jnp.concatenate([jnp.concatenate([fr, -fi], axis=1),
                            jnp.concatenate([fi, fr], axis=1)], axis=0).astype(BF16)
    fker = jnp.concatenate([cr, -sr], axis=0).astype(BF16)
    er, ei = cr[:seq_len], sr[:seq_len]
    finv = (jnp.concatenate([jnp.concatenate([er, -ei], axis=1),
                             jnp.concatenate([ei, er], axis=1)], axis=0) * (1.0 / n)).astype(BF16)
    full = lambda shape: pl.BlockSpec(shape, lambda p: (0,) * len(shape))
    out = pl.pallas_call(
        functools.partial(_dft_conv_kernel, n=n),
        out_shape=jax.ShapeDtypeStruct((npair, 2 * seq_len, ch), F32),
        grid=(npair,),
        in_specs=[full((2 * n, 2 * seq_len)), full((2 * n, n)), full((2 * seq_len, 2 * n)),
                  full((n, ch)), full((1, 2 * D_HY)),
                  pl.BlockSpec((1, 2 * seq_len, ch), lambda p: (p, 0, 0))],
        out_specs=pl.BlockSpec((1, 2 * seq_len, ch), lambda p: (p, 0, 0)),
        compiler_params=_cparams(1),
        name="dft_conv",
    )(ffwd, fker, finv, kern, asum, wprod.reshape(npair, 2 * seq_len, ch))
    return out.reshape(bsz, seq_len, ch)


def _group_rms(y, w, width):
    parts = []
    for g in range(y.shape[1] // width):
        seg = y[:, g * width:(g + 1) * width]
        ms = jnp.mean(seg * seg, axis=-1, keepdims=True)
        parts.append(seg * lax.rsqrt(ms + EPS) * w[:, g * width:(g + 1) * width])
    return parts


def _outproj_kernel(x_ref, yf_ref, yb_ref, xs_ref, z_ref, of_ref, ob_ref, gg_ref, cv_ref, wp_ref, x0_ref,
                    gy_ref, d_ref, snw_ref, hnw_ref, hyb_ref, gate_ref, w_ref, fnw_ref, o_ref, *, last):
    y_a = (yf_ref[...] + yb_ref[...] + d_ref[...] * xs_ref[...]) * _silu(z_ref[...])
    acc = None
    for g, seg in enumerate(_group_rms(y_a, snw_ref[...], SSD_GW)):
        term = jnp.dot(seg.astype(BF16), w_ref[g * SSD_GW:(g + 1) * SSD_GW, :], preferred_element_type=F32)
        acc = term if acc is None else acc + term
    o_hg = of_ref[...] + ob_ref[...]
    gate_hg = _silu(gg_ref[...])
    for h, seg in enumerate(_group_rms(o_hg, hnw_ref[...], HG_HEAD_DIM)):
        lo = h * HG_HEAD_DIM
        y_b = seg * gate_hg[:, lo:lo + HG_HEAD_DIM]
        acc = acc + jnp.dot(y_b.astype(BF16), w_ref[D_SSD + lo:D_SSD + lo + HG_HEAD_DIM, :],
                            preferred_element_type=F32)
    wp = wp_ref[...]
    y_c = x0_ref[...] * (cv_ref[...] + hyb_ref[...] * wp) * _silu(gy_ref[...])
    acc = acc + jnp.dot(y_c.astype(BF16), w_ref[D_SSD + D_HG:, :], preferred_element_type=F32)
    xn = x_ref[...] + gate_ref[0] * acc
    if last:
        ms = jnp.mean(xn * xn, axis=-1, keepdims=True)
        xn = xn * lax.rsqrt(ms + EPS) * fnw_ref[...]
    o_ref[...] = xn


def _outproj(x2d, yf, yb, xbc_act, gates, of, ob, conv, wprod, hx0, d_full, ssd_nw, hg_nw, hy_bias, gate,
             w_out, final_nw, seq_len, last):
    m, d = x2d.shape
    tm = ROW_TILE
    nb = gate.shape[0]
    per = seq_len // tm
    mod_map = (lambda i: (0, 0, 0)) if nb == 1 else (lambda i: (i // per, 0, 0))
    row = lambda width, cb=0: pl.BlockSpec((tm, width), lambda i, cb=cb: (i, cb))
    vec = lambda width: pl.BlockSpec((1, width), lambda i: (0, 0))
    f2 = lambda a: a.reshape(m, a.shape[-1])
    return pl.pallas_call(
        functools.partial(_outproj_kernel, last=last),
        out_shape=jax.ShapeDtypeStruct((m, d), F32),
        grid=(m // tm,),
        in_specs=[row(d), row(D_SSD), row(D_SSD), row(D_SSD), row(D_SSD),
                  row(D_HG), row(D_HG), row(D_HG, D_SSD // D_HG),
                  row(D_HY), row(D_HY), row(D_HY), row(D_HY, (D_SSD + D_HG) // D_HY),
                  vec(D_SSD), vec(D_SSD), vec(D_HG), vec(D_HY),
                  pl.BlockSpec((1, 1, d), mod_map),
                  pl.BlockSpec((D_MIX, d), lambda i: (0, 0), pipeline_mode=pl.Buffered(1)),
                  vec(d)],
        out_specs=pl.BlockSpec((tm, d), lambda i: (i, 0)),
        compiler_params=_cparams(1, VMEM_LIMIT),
        name="outproj_final" if last else "outproj",
    )(x2d, f2(yf), f2(yb), f2(xbc_act), f2(gates), f2(of), f2(ob), f2(gates), f2(conv), f2(wprod), f2(hx0),
      f2(gates), d_full, ssd_nw.reshape(1, -1), hg_nw.reshape(1, -1), hy_bias.reshape(1, -1), gate,
      w_out.astype(BF16), final_nw.reshape(1, -1))


def _mix_tokens(x_seq, l, p, norm_w, scale, shift, gate, wp, states, rows, width, need_out, last, final_nw):
    bsz, seq_len, d = x_seq.shape
    m = bsz * seq_len
    x2d = x_seq.reshape(m, d)
    xbc, hy, hg, gates, dtp = _inproj(x2d, norm_w, scale, shift, wp, seq_len)
    r3 = lambda a: a.reshape(bsz, seq_len, a.shape[-1])
    xbc, hy, hg, gates = r3(xbc), r3(hy), r3(hg), r3(gates)
    dtp = r3(dtp)

    xbc_act = _ssd_conv(xbc, p["ssd_conv_w"], p["ssd_conv_b"])
    ssd_yf, ssd_hf = _ssd_scan(xbc_act, dtp[..., :SSD_HEADS], p["ssd_dt_bias"][0], p["ssd_a_log"][0],
                               states[0], rev=False)
    ssd_yb, ssd_hb = _ssd_scan(xbc_act, dtp[..., SSD_HEADS:2 * SSD_HEADS], p["ssd_dt_bias"][1],
                               p["ssd_a_log"][1], states[1], rev=True)
    hg_of, hg_sf = _gla_scan(hg, p["hg_lb_logits"][0], states[2], rows, width, l, rev=False)
    hg_ob, hg_sb = _gla_scan(hg, p["hg_lb_logits"][1], states[3], rows, width, l, rev=True)
    new_states = (ssd_hf, ssd_hb, hg_sf, hg_sb)
    if not need_out:
        return new_states, None

    wprod, hx0 = _hy_conv(hy, p["hy_conv_w"], p["hy_conv_b"])
    kern, asum = _hyena_filter(seq_len, p["hy_filt_w1"], p["hy_filt_b1"], p["hy_filt_w2"], p["hy_filt_b2"],
                               p["hy_filt_freq"], p["hy_filt_w3"])
    conv = (_dft_conv if seq_len <= DIRECT_DFT_MAX_L else _fft_conv)(wprod, kern, asum)
    d_full = jnp.repeat(p["ssd_d"], SSD_HEAD_DIM).reshape(1, D_SSD)
    x_new = _outproj(x2d, ssd_yf, ssd_yb, xbc_act, gates, hg_of, hg_ob, conv, wprod, hx0, d_full,
                     p["ssd_norm_w"], p["hg_norm_w"], p["hy_bias"], gate, p["w_out"], final_nw, seq_len, last)
    return new_states, x_new.reshape(bsz, seq_len, d)


def kernel(x, c, ctx, c_ctx, ada_w, ada_b, norm_w, w_in, ssd_conv_w, ssd_conv_b, ssd_dt_bias, ssd_a_log,
           ssd_d, ssd_norm_w, hg_lb_logits, hg_norm_w, hy_conv_w, hy_conv_b, hy_filt_w1, hy_filt_b1,
           hy_filt_w2, hy_filt_b2, hy_filt_freq, hy_filt_w3, hy_bias, w_out, final_norm_w):
    bsz, seq, d = x.shape
    depth = ada_w.shape[0]
    rows = seq // GRID_W
    ctx_len = ctx.shape[1]
    assert bsz % 2 == 0 and rows % HG_CHUNK == 0 and seq % ROW_TILE == 0 and ctx_len % ROW_TILE == 0

    nmod = -(-(bsz + 1) // 8) * 8
    cvec = jnp.zeros((nmod, d), F32).at[:bsz].set(c).at[bsz].set(c_ctx)
    mods = _adaln(cvec, ada_w, ada_b)

    zero_states = (jnp.zeros((bsz, SSD_GROUPS, SSD_STATE, SSD_GW), F32),) * 2 + (
        jnp.zeros((bsz, HG_HEADS, HG_HEAD_DIM, HG_HEAD_DIM), F32),) * 2
    xc = ctx
    for l in range(depth):
        last = l == depth - 1
        p = {"ssd_conv_w": ssd_conv_w[l], "ssd_conv_b": ssd_conv_b[l], "ssd_dt_bias": ssd_dt_bias[l],
             "ssd_a_log": ssd_a_log[l], "ssd_d": ssd_d[l], "ssd_norm_w": ssd_norm_w[l],
             "hg_lb_logits": hg_lb_logits, "hg_norm_w": hg_norm_w[l], "hy_conv_w": hy_conv_w[l],
             "hy_conv_b": hy_conv_b[l], "hy_filt_w1": hy_filt_w1[l], "hy_filt_b1": hy_filt_b1[l],
             "hy_filt_w2": hy_filt_w2[l], "hy_filt_b2": hy_filt_b2[l], "hy_filt_freq": hy_filt_freq[l],
             "hy_filt_w3": hy_filt_w3[l], "hy_bias": hy_bias[l], "w_out": w_out[l]}
        wp = _permute_w_in(w_in[l])
        mod = mods[l]
        shift, scale, gate = (mod[:bsz, i * d:(i + 1) * d].reshape(bsz, 1, d) for i in range(3))
        shift_c, scale_c, gate_c = (mod[bsz:bsz + 1, i * d:(i + 1) * d].reshape(1, 1, d) for i in range(3))
        ctx_states, xc_new = _mix_tokens(xc, l, p, norm_w[l], scale_c, shift_c, gate_c, wp, zero_states,
                                         ctx_len, 1, not last, False, final_norm_w)
        _, x = _mix_tokens(x, l, p, norm_w[l], scale, shift, gate, wp, ctx_states, rows, GRID_W, True, last,
                           final_norm_w)
        if not last:
            xc = xc_new
    return x
```
